```python
import jax, jax.numpy as jnp
from jax import lax
import numpy as np

D_MODEL = 1024
BATCH = 2
SEQ = 8192
DEPTH = 4

CHUNK = 64
HEAD_DIM = 64
N_FOX_HEADS = D_MODEL // (2 * HEAD_DIM)
N_SB_HEADS = D_MODEL // (2 * HEAD_DIM)
N_HEADS = N_FOX_HEADS + N_SB_HEADS
D_FOX = N_FOX_HEADS * HEAD_DIM
D_SB = N_SB_HEADS * HEAD_DIM
D_MIX = D_FOX + D_SB
QBLOCK = 128
N_EXPERTS = 16
N_GROUPS = 4
EXPERTS_PER_GROUP = N_EXPERTS // N_GROUPS
TOP_K = 2
D_EXPERT = D_MODEL
EXPERT_BLOCK = 128
DEEPNORM_ALPHA = (2 * DEPTH) ** 0.25
DEEPNORM_BETA = (8 * DEPTH) ** -0.25
LN_EPS = 1e-5
RMS_EPS = 1e-6
N_MOD = 6
SPLIT_POINTS = (D_FOX, 2 * D_FOX, 3 * D_FOX, 4 * D_FOX, 4 * D_FOX + N_FOX_HEADS,
                4 * D_FOX + N_FOX_HEADS + D_SB, 4 * D_FOX + N_FOX_HEADS + 2 * D_SB)
N_IN = 4 * D_FOX + N_FOX_HEADS + 3 * D_SB

kernel_name = "hybrid_fox_stickbreak_grouped_moe_deepnorm"


def layer_norm(x, g, b):
    xf = x.astype(jnp.float32)
    mu = jnp.mean(xf, axis=-1, keepdims=True)
    var = jnp.mean(jnp.square(xf - mu), axis=-1, keepdims=True)
    return ((xf - mu) * lax.rsqrt(var + LN_EPS) * g + b).astype(x.dtype)


def _split_heads(t, n_heads):
    B, S, _ = t.shape
    return t.reshape(B, S, n_heads, HEAD_DIM).transpose(0, 2, 1, 3)


def forgetting_attention(q, k, v, log_f):
    B, H, S, Dh = q.shape
    nb = S // QBLOCK
    F = jnp.cumsum(log_f, axis=-1)
    q_blocks = q.reshape(B, H, nb, QBLOCK, Dh).transpose(2, 0, 1, 3, 4)
    F_blocks = F.reshape(B, H, nb, QBLOCK).transpose(2, 0, 1, 3)
    k_pos = jnp.arange(S)
    scale = HEAD_DIM ** -0.5

    def block(args):
        qb, Fb, start = args
        logits = jnp.einsum('bhqd,bhkd->bhqk', qb, k).astype(jnp.float32) * scale
        logits = logits + (Fb[..., :, None] - F[:, :, None, :])
        q_pos = start + jnp.arange(QBLOCK)
        causal = k_pos[None, :] <= q_pos[:, None]
        logits = jnp.where(causal, logits, -jnp.inf)
        p = jax.nn.softmax(logits, axis=-1)
        return jnp.einsum('bhqk,bhkd->bhqd', p.astype(v.dtype), v)

    out = lax.map(block, (q_blocks, F_blocks, jnp.arange(nb) * QBLOCK))
    return out.transpose(1, 2, 0, 3, 4).reshape(B, H, S, Dh)


def stick_breaking_attention(q, k, v):
    B, H, S, Dh = q.shape
    nb = S // QBLOCK
    q_blocks = q.reshape(B, H, nb, QBLOCK, Dh).transpose(2, 0, 1, 3, 4)
    k_pos = jnp.arange(S)
    scale = HEAD_DIM ** -0.5

    def block(args):
        qb, start = args
        z = jnp.einsum('bhqd,bhkd->bhqk', qb, k).astype(jnp.float32) * scale
        q_pos = start + jnp.arange(QBLOCK)
        strict = k_pos[None, :] < q_pos[:, None]
        log_beta = jax.nn.log_sigmoid(z)
        log_keep = jnp.where(strict, jax.nn.log_sigmoid(-z), 0.0)
        later = lax.cumsum(log_keep, axis=3, reverse=True) - log_keep
        w = jnp.where(strict, jnp.exp(log_beta + later), 0.0)
        return jnp.einsum('bhqk,bhkd->bhqd', w.astype(v.dtype), v)

    out = lax.map(block, (q_blocks, jnp.arange(nb) * QBLOCK))
    return out.transpose(1, 2, 0, 3, 4).reshape(B, H, S, Dh)


def hybrid_mixer(u, w_in, b_forget, out_norm_g, w_out):
    B, S, _ = u.shape
    proj = u @ w_in
    fq, fk, fv, fgate, fforget, sq, sk, sv = jnp.split(proj, SPLIT_POINTS, axis=-1)
    log_f = jax.nn.log_sigmoid((fforget + b_forget).astype(jnp.float32)).transpose(0, 2, 1)
    o_fox = forgetting_attention(_split_heads(fq, N_FOX_HEADS), _split_heads(fk, N_FOX_HEADS),
                                 _split_heads(fv, N_FOX_HEADS), log_f)
    o_sb = stick_breaking_attention(_split_heads(sq, N_SB_HEADS), _split_heads(sk, N_SB_HEADS),
                                    _split_heads(sv, N_SB_HEADS))
    o = jnp.concatenate([o_fox, o_sb], axis=1).transpose(0, 2, 1, 3)
    of = o.astype(jnp.float32)
    of = of * lax.rsqrt(jnp.mean(jnp.square(of), axis=-1, keepdims=True) + RMS_EPS)
    o = (of * out_norm_g.reshape(N_HEADS, HEAD_DIM)).astype(u.dtype).reshape(B, S, D_MIX)
    o = jnp.concatenate([o[..., :D_FOX] * jax.nn.sigmoid(fgate), o[..., D_FOX:]], axis=-1)
    return o @ w_out


def grouped_moe(h, w_router, b_router, w_gate, w_up, w_down):
    T, D = h.shape
    probs = jax.nn.softmax((h @ w_router).astype(jnp.float32), axis=-1)
    sel = probs + b_router
    group_score = lax.top_k(sel.reshape(T, N_GROUPS, EXPERTS_PER_GROUP), TOP_K)[0].sum(-1)
    best_group = jnp.argmax(group_score, axis=-1)
    in_group = (jnp.arange(N_EXPERTS) // EXPERTS_PER_GROUP)[None, :] == best_group[:, None]
    _, expert_idx = lax.top_k(jnp.where(in_group, sel, -jnp.inf), TOP_K)
    gate_w = jnp.take_along_axis(probs, expert_idx, axis=-1)
    gate_w = gate_w / jnp.sum(gate_w, axis=-1, keepdims=True)

    A = T * TOP_K
    flat_e = expert_idx.reshape(-1)
    flat_tok = jnp.repeat(jnp.arange(T), TOP_K)
    flat_w = gate_w.reshape(-1)
    order = jnp.argsort(flat_e)
    sorted_e = flat_e[order]
    sorted_tok = flat_tok[order]
    sorted_w = flat_w[order]
    counts = jnp.bincount(flat_e, length=N_EXPERTS)
    padded = (counts + EXPERT_BLOCK - 1) // EXPERT_BLOCK * EXPERT_BLOCK
    starts = jnp.cumsum(counts) - counts
    pad_ends = jnp.cumsum(padded)
    pad_starts = pad_ends - padded
    dest = pad_starts[sorted_e] + (jnp.arange(A) - starts[sorted_e])
    n_blocks = A // EXPERT_BLOCK + N_EXPERTS
    P = n_blocks * EXPERT_BLOCK
    row_tok = jnp.zeros((P,), jnp.int32).at[dest].set(sorted_tok.astype(jnp.int32))
    block_e = jnp.minimum(jnp.searchsorted(pad_ends, jnp.arange(n_blocks) * EXPERT_BLOCK, side='right'),
                          N_EXPERTS - 1)
    xs = h[row_tok].reshape(n_blocks, EXPERT_BLOCK, D)

    def expert_block(args):
        xb, e = args
        return (jax.nn.silu(xb @ w_gate[e]) * (xb @ w_up[e])) @ w_down[e]

    yb = lax.map(expert_block, (xs, block_e)).reshape(P, D)
    y_assign = yb[dest] * sorted_w[:, None].astype(h.dtype)
    return jax.ops.segment_sum(y_assign, sorted_tok, num_segments=T)


def setup_inputs(seed: int = 0) -> dict:
    key = jax.random.key(seed)
    ks = jax.random.split(key, 18)
    f32 = jnp.float32
    nrm = lambda k, shape, s: jax.random.normal(k, shape, f32) * s
    col_scale = jnp.concatenate([jnp.ones((2 * D_FOX,), f32), jnp.full((D_FOX,), DEEPNORM_BETA, f32),
                                 jnp.ones((D_FOX + N_FOX_HEADS + 2 * D_SB,), f32),
                                 jnp.full((D_SB,), DEEPNORM_BETA, f32)])
    return {
        "x": nrm(ks[0], (BATCH, SEQ, D_MODEL), 1.0),
        "c": nrm(ks[1], (BATCH, D_MODEL), 1.0),
        "w_mod": nrm(ks[2], (DEPTH, D_MODEL, N_MOD * D_MODEL), 0.5 * D_MODEL ** -0.5),
        "b_mod": nrm(ks[3], (DEPTH, N_MOD * D_MODEL), 0.02),
        "w_in": nrm(ks[4], (DEPTH, D_MODEL, N_IN), D_MODEL ** -0.5) * col_scale,
        "b_forget": jax.random.uniform(ks[5], (DEPTH, N_FOX_HEADS), f32, 1.0, 5.0),
        "out_norm_g": 1.0 + nrm(ks[6], (DEPTH, D_MIX), 0.02),
        "w_out": nrm(ks[7], (DEPTH, D_MIX, D_MODEL), D_MIX ** -0.5 * DEEPNORM_BETA),
        "ln1_g": 1.0 + nrm(ks[8], (DEPTH, D_MODEL), 0.02),
        "ln1_b": nrm(ks[9], (DEPTH, D_MODEL), 0.02),
        "w_router": nrm(ks[10], (D_MODEL, N_EXPERTS), D_MODEL ** -0.5),
        "b_router": nrm(ks[11], (N_EXPERTS,), 0.01),
        "w_gate": nrm(ks[12], (DEPTH, N_EXPERTS, D_MODEL, D_EXPERT), D_MODEL ** -0.5),
        "w_up": nrm(ks[13], (DEPTH, N_EXPERTS, D_MODEL, D_EXPERT), D_MODEL ** -0.5 * DEEPNORM_BETA),
        "w_down": nrm(ks[14], (DEPTH, N_EXPERTS, D_EXPERT, D_MODEL), D_EXPERT ** -0.5 * DEEPNORM_BETA),
        "ln2_g": 1.0 + nrm(ks[15], (DEPTH, D_MODEL), 0.02),
        "ln2_b": nrm(ks[16], (DEPTH, D_MODEL), 0.02),
    }


def reference(x, c, w_mod, b_mod, w_in, b_forget, out_norm_g, w_out, ln1_g, ln1_b,
              w_router, b_router, w_gate, w_up, w_down, ln2_g, ln2_b):
    B, S, D = x.shape
    cond = jax.nn.silu(c)
    for l in range(DEPTH):
        mod = (cond @ w_mod[l] + b_mod[l])[:, None, :]
        sh_a, sc_a, g_a, sh_f, sc_f, g_f = jnp.split(mod, N_MOD, axis=-1)
        u = x * (1.0 + sc_a) + sh_a
        y = hybrid_mixer(u, w_in[l], b_forget[l], out_norm_g[l], w_out[l])
        x = layer_norm(DEEPNORM_ALPHA * x + (1.0 + g_a) * y, ln1_g[l], ln1_b[l])
        u = x * (1.0 + sc_f) + sh_f
        y = grouped_moe(u.reshape(B * S, D), w_router, b_router, w_gate[l], w_up[l], w_down[l]).reshape(B, S, D)
        x = layer_norm(DEEPNORM_ALPHA * x + (1.0 + g_f) * y, ln2_g[l], ln2_b[l])
    return x
```

```python
import functools

import jax
import jax.numpy as jnp
from jax import lax
from jax.experimental import pallas as pl
from jax.experimental.pallas import tpu as pltpu

F32 = jnp.float32
BF16 = jnp.bfloat16

HEAD_DIM = 64
LANES = 128
N_FOX_HEADS = 8
N_SB_HEADS = 8
N_EXPERTS = 16
N_GROUPS = 4
EXPERTS_PER_GROUP = 4
N_MOD = 6
LN_EPS = 1e-5
RMS_EPS = 1e-6
ROW_TILE = 8
VMEM_LIMIT = 56 * 1024 * 1024
NEG_BIG = -1e30
SB_DEAD = -105.0

_NT = (((1,), (1,)), ((), ()))


def _dot(a, b):
    return jnp.dot(a, b, preferred_element_type=F32)


def _dot_nt(a, b):
    return lax.dot_general(a, b, _NT, preferred_element_type=F32)


def _split2(x):
    hi = x.astype(BF16)
    lo = (x - hi.astype(F32)).astype(BF16)
    return hi, lo


def _split3(x):
    hi = x.astype(BF16)
    r = x - hi.astype(F32)
    mid = r.astype(BF16)
    lo = (r - mid.astype(F32)).astype(BF16)
    return hi, mid, lo


def _log_sigmoid(x):
    return jnp.minimum(x, 0.0) - jnp.log1p(jnp.exp(-jnp.abs(x)))


def _layer_norm(z, g, b):
    mu = jnp.mean(z, axis=-1, keepdims=True)
    zc = z - mu
    var = jnp.mean(zc * zc, axis=-1, keepdims=True)
    return zc * lax.rsqrt(var + LN_EPS) * g + b


def _params(sem, vmem=VMEM_LIMIT):
    return pltpu.CompilerParams(dimension_semantics=sem, vmem_limit_bytes=vmem)


def _mod_kernel(ct_ref, w_ref, b_ref, o_ref):
    ct = ct_ref[...]
    cond = ct * jax.nn.sigmoid(ct)
    w = w_ref[0]
    rows = [jnp.sum(cond[:, b:b + 1] * w, axis=0, keepdims=True) for b in range(ct.shape[1])]
    o_ref[0] = jnp.concatenate(rows, axis=0) + b_ref[0]


def _modulation(c, w_mod, b_mod, tn=512):
    depth, d, n = w_mod.shape
    bsz = c.shape[0]
    return pl.pallas_call(
        _mod_kernel,
        out_shape=jax.ShapeDtypeStruct((depth, bsz, n), F32),
        grid=(depth, n // tn),
        in_specs=[pl.BlockSpec((d, bsz), lambda l, j: (0, 0)),
                  pl.BlockSpec((1, d, tn), lambda l, j: (l, 0, j)),
                  pl.BlockSpec((1, 1, tn), lambda l, j: (l, 0, j))],
        out_specs=pl.BlockSpec((1, bsz, tn), lambda l, j: (l, 0, j)),
        compiler_params=_params(("parallel", "parallel")),
        name="modulation",
    )(c.T, w_mod, b_mod.reshape(depth, 1, n))


def _inproj_kernel(x_ref, sc_ref, sh_ref, wq_ref, wkt_ref, wv_ref, wg_ref, wf_ref, bf_ref,
                   q_ref, kt_ref, v_ref, g_ref, lf_ref):
    u = x_ref[0] * (1.0 + sc_ref[0]) + sh_ref[0]
    u_hi, u_lo = _split2(u)
    q_ref[0] = _dot(u_hi, wq_ref[...]).astype(BF16)
    v_ref[0] = _dot(u_hi, wv_ref[...]).astype(BF16)
    g_ref[0] = _dot(u_hi, wg_ref[...])
    kt_ref[0] = _dot_nt(wkt_ref[...], u_hi).astype(BF16)
    wf = wf_ref[...]
    a = _dot_nt(wf, u_hi)
    b = _dot_nt(wf[0:16], u_lo)
    nf = N_FOX_HEADS
    fl = a[0:nf] + (a[16:16 + nf] + b[0:nf]) + bf_ref[...]
    lf_ref[0] = _log_sigmoid(fl)


def _input_projection(x, sc, sh, wq, wkt, wv, wg, wf, bfg, ts):
    bsz, s, d = x.shape
    dq, dg = wq.shape[1], wg.shape[1]
    nf = N_FOX_HEADS
    full = lambda shape: pl.BlockSpec(shape, lambda b, i: (0,) * len(shape))
    return pl.pallas_call(
        _inproj_kernel,
        out_shape=(jax.ShapeDtypeStruct((bsz, s, dq), BF16),
                   jax.ShapeDtypeStruct((bsz, dq, s), BF16),
                   jax.ShapeDtypeStruct((bsz, s, dq), BF16),
                   jax.ShapeDtypeStruct((bsz, s, dg), F32),
                   jax.ShapeDtypeStruct((bsz, nf, s), F32)),
        grid=(bsz, s // ts),
        in_specs=[pl.BlockSpec((1, ts, d), lambda b, i: (b, i, 0)),
                  pl.BlockSpec((1, 1, d), lambda b, i: (b, 0, 0)),
                  pl.BlockSpec((1, 1, d), lambda b, i: (b, 0, 0)),
                  full(wq.shape), full(wkt.shape), full(wv.shape), full(wg.shape),
                  full(wf.shape), full(bfg.shape)],
        out_specs=(pl.BlockSpec((1, ts, dq), lambda b, i: (b, i, 0)),
                   pl.BlockSpec((1, dq, ts), lambda b, i: (b, 0, i)),
                   pl.BlockSpec((1, ts, dq), lambda b, i: (b, i, 0)),
                   pl.BlockSpec((1, ts, dg), lambda b, i: (b, i, 0)),
                   pl.BlockSpec((1, nf, ts), lambda b, i: (b, 0, i))),
        compiler_params=_params(("parallel", "parallel")),
        name="input_projection",
    )(x, sc, sh, wq, wkt, wv, wg, wf, bfg)


def _cumsum_kernel(lf_ref, f_ref, *, chunk):
    rows, s = lf_ref.shape
    r = lax.broadcasted_iota(jnp.int32, (chunk, chunk), 0)
    c = lax.broadcasted_iota(jnp.int32, (chunk, chunk), 1)
    tri = jnp.where(r <= c, 1.0, 0.0).astype(BF16)
    carry = jnp.zeros((rows, 1), F32)
    for k in range(s // chunk):
        hi, mid, lo = _split3(lf_ref[:, k * chunk:(k + 1) * chunk])
        y = _dot(jnp.concatenate([hi, mid, lo], axis=0), tri)
        cs = (y[2 * rows:3 * rows] + y[rows:2 * rows]) + y[0:rows] + carry
        f_ref[:, k * chunk:(k + 1) * chunk] = cs
        carry = cs[:, chunk - 1:chunk]


def _forget_cumsum(log_f, chunk=256):
    bsz, nf, s = log_f.shape
    out = pl.pallas_call(
        functools.partial(_cumsum_kernel, chunk=chunk),
        out_shape=jax.ShapeDtypeStruct((bsz * nf, s), F32),
        name="forget_cumsum",
    )(log_f.reshape(bsz * nf, s))
    return out.reshape(bsz, nf, s)


def _head_pair(q):
    lane = lax.broadcasted_iota(jnp.int32, (1, LANES), 1)
    first = lane < HEAD_DIM
    zero = jnp.zeros_like(q)
    return first, (jnp.where(first, q, zero), jnp.where(first, zero, q))


def _pair_rms_norm(first, o, gain):
    sq = o * o
    ss0 = jnp.sum(jnp.where(first, sq, 0.0), axis=-1, keepdims=True)
    ss1 = jnp.sum(jnp.where(first, 0.0, sq), axis=-1, keepdims=True)
    inv = jnp.where(first, lax.rsqrt(ss0 / HEAD_DIM + RMS_EPS), lax.rsqrt(ss1 / HEAD_DIM + RMS_EPS))
    return o * inv * gain


def _fox_kernel(q_ref, kt_ref, v_ref, g_ref, fs_ref, ft_ref, ng_ref, o_ref, *, blk):
    i = pl.program_id(2)
    first, qh = _head_pair(q_ref[0])
    ft = ft_ref[0, 0]
    f_t = (ft[:, 0:1], ft[:, 1:2])

    def block(j, state, diag):
        k0 = pl.multiple_of(j * blk, blk)
        kt = kt_ref[0, :, pl.ds(k0, blk)]
        v = v_ref[0, pl.ds(k0, blk), :]
        fs = fs_ref[0, 0, :, pl.ds(k0, blk)]
        out = []
        for h in range(2):
            m, l, acc = state[h]
            s = _dot(qh[h], kt) + (f_t[h] - fs[h:h + 1])
            if diag:
                row = lax.broadcasted_iota(jnp.int32, (blk, blk), 0)
                col = lax.broadcasted_iota(jnp.int32, (blk, blk), 1)
                s = jnp.where(col <= row, s, NEG_BIG)
            m_new = jnp.maximum(m, jnp.max(s, axis=-1, keepdims=True))
            p = jnp.exp(s - m_new)
            alpha = jnp.exp(m - m_new)
            l_new = alpha * l + jnp.sum(p, axis=-1, keepdims=True)
            acc_new = alpha * acc + _dot(p.astype(BF16), v)
            out.append((m_new, l_new, acc_new))
        return tuple(out)

    init1 = (jnp.full((blk, 1), NEG_BIG, F32), jnp.zeros((blk, 1), F32), jnp.zeros((blk, LANES), F32))
    state = lax.fori_loop(0, i, lambda j, st: block(j, st, False), (init1, init1))
    state = block(i, state, True)
    (_, l0, a0), (_, l1, a1) = state
    o = jnp.where(first, a0 / l0, a1 / l1)
    o = _pair_rms_norm(first, o, ng_ref[...]) * jax.nn.sigmoid(g_ref[0])
    o_ref[0] = o.astype(BF16)


def _fox_attention(q, kt, v, gate, f_s, f_t, gain, blk):
    bsz, s, _ = q.shape
    npair = N_FOX_HEADS // 2
    return pl.pallas_call(
        functools.partial(_fox_kernel, blk=blk),
        out_shape=jax.ShapeDtypeStruct((bsz, s, npair * LANES), BF16),
        grid=(bsz, npair, s // blk),
        in_specs=[pl.BlockSpec((1, blk, LANES), lambda b, p, i: (b, i, p)),
                  pl.BlockSpec((1, LANES, s), lambda b, p, i: (b, p, 0)),
                  pl.BlockSpec((1, s, LANES), lambda b, p, i: (b, 0, p)),
                  pl.BlockSpec((1, blk, LANES), lambda b, p, i: (b, i, p)),
                  pl.BlockSpec((1, 1, 2, s), lambda b, p, i: (b, p, 0, 0)),
                  pl.BlockSpec((1, 1, blk, 2), lambda b, p, i: (b, p, i, 0)),
                  pl.BlockSpec((1, LANES), lambda b, p, i: (0, p))],
        out_specs=pl.BlockSpec((1, blk, LANES), lambda b, p, i: (b, i, p)),
        compiler_params=_params(("parallel", "parallel", "arbitrary")),
        name="fox_attention",
    )(q, kt, v, gate, f_s, f_t, gain)


def _sb_kernel(q_ref, kt_ref, v_ref, ng_ref, o_ref, *, blk):
    i = pl.program_id(2)
    first, qh = _head_pair(q_ref[0])
    row = lax.broadcasted_iota(jnp.int32, (blk, blk), 0)
    col = lax.broadcasted_iota(jnp.int32, (blk, blk), 1)
    after = jnp.where(row > col, 1.0, 0.0).astype(BF16)

    def block(j, state, diag):
        k0 = pl.multiple_of(j * blk, blk)
        kt = kt_ref[0, :, pl.ds(k0, blk)]
        v = v_ref[0, pl.ds(k0, blk), :]
        out = []
        for h in range(2):
            carry, acc = state[h]
            z = _dot(qh[h], kt)
            nz = -z
            sp = jnp.log1p(jnp.exp(jnp.minimum(z, nz)))
            log_beta = jnp.minimum(z, 0.0) - sp
            log_keep = jnp.minimum(nz, 0.0) - sp
            if diag:
                log_keep = jnp.where(col < row, log_keep, 0.0)
            hi, lo = _split2(log_keep)
            later = _dot(hi, after) + _dot(lo, after) + carry
            w = jnp.exp(log_beta + later)
            if diag:
                w = jnp.where(col < row, w, 0.0)
            out.append((carry + jnp.sum(log_keep, axis=-1, keepdims=True),
                        acc + _dot(w.astype(BF16), v)))
        return tuple(out)

    init1 = (jnp.zeros((blk, 1), F32), jnp.zeros((blk, LANES), F32))
    state = block(i, (init1, init1), True)

    def alive(carry):
        (j, st) = carry
        live = jnp.maximum(jnp.max(st[0][0]), jnp.max(st[1][0])) >= SB_DEAD
        return jnp.logical_and(j >= 0, live)

    def step(carry):
        (j, st) = carry
        return (j - 1, block(j, st, False))

    _, state = lax.while_loop(alive, step, (i - 1, state))
    o = jnp.where(first, state[0][1], state[1][1])
    o_ref[0] = _pair_rms_norm(first, o, ng_ref[...]).astype(BF16)


def _sb_attention(q, kt, v, gain, blk):
    bsz, s, _ = q.shape
    npair = N_SB_HEADS // 2
    off = N_FOX_HEADS // 2
    return pl.pallas_call(
        functools.partial(_sb_kernel, blk=blk),
        out_shape=jax.ShapeDtypeStruct((bsz, s, npair * LANES), BF16),
        grid=(bsz, npair, s // blk),
        in_specs=[pl.BlockSpec((1, blk, LANES), lambda b, p, i: (b, i, p + off)),
                  pl.BlockSpec((1, LANES, s), lambda b, p, i: (b, p + off, 0)),
                  pl.BlockSpec((1, s, LANES), lambda b, p, i: (b, 0, p + off)),
                  pl.BlockSpec((1, LANES), lambda b, p, i: (0, p + off))],
        out_specs=pl.BlockSpec((1, blk, LANES), lambda b, p, i: (b, i, p)),
        compiler_params=_params(("parallel", "parallel", "arbitrary")),
        name="sb_attention",
    )(q, kt, v, gain)


def _top2_of4(r):
    m1 = jnp.maximum(jnp.maximum(r[0], r[1]), jnp.maximum(r[2], r[3]))
    i1 = jnp.where(r[0] == m1, 0, jnp.where(r[1] == m1, 1, jnp.where(r[2] == m1, 2, 3)))
    rest = [jnp.where(i1 == k, -jnp.inf, r[k]) for k in range(4)]
    m2 = jnp.maximum(jnp.maximum(rest[0], rest[1]), jnp.maximum(rest[2], rest[3]))
    i2 = jnp.where(rest[0] == m2, 0, jnp.where(rest[1] == m2, 1, jnp.where(rest[2] == m2, 2, 3)))
    return m1, i1, m2, i2


def _route(logits, b_router):
    mx = jnp.max(logits, axis=0, keepdims=True)
    ex = jnp.exp(logits - mx)
    probs = ex / jnp.sum(ex, axis=0, keepdims=True)
    sel = probs + b_router
    groups = []
    for g in range(N_GROUPS):
        rows = [sel[g * EXPERTS_PER_GROUP + k:g * EXPERTS_PER_GROUP + k + 1] for k in range(EXPERTS_PER_GROUP)]
        groups.append(_top2_of4(rows))
    score = [m1 + m2 for (m1, _, m2, _) in groups]
    best = jnp.maximum(jnp.maximum(score[0], score[1]), jnp.maximum(score[2], score[3]))
    gid = jnp.where(score[0] == best, 0, jnp.where(score[1] == best, 1, jnp.where(score[2] == best, 2, 3)))
    e1 = jnp.zeros_like(gid)
    e2 = jnp.zeros_like(gid)
    for g in range(N_GROUPS):
        e1 = jnp.where(gid == g, g * EXPERTS_PER_GROUP + groups[g][1], e1)
        e2 = jnp.where(gid == g, g * EXPERTS_PER_GROUP + groups[g][3], e2)
    eid = lax.broadcasted_iota(jnp.int32, logits.shape, 0)
    p1 = jnp.sum(jnp.where(eid == e1, probs, 0.0), axis=0, keepdims=True)
    p2 = jnp.sum(jnp.where(eid == e2, probs, 0.0), axis=0, keepdims=True)
    tot = p1 + p2
    return jnp.concatenate([e1, e2], axis=0), jnp.concatenate([p1 / tot, p2 / tot], axis=0)


def _outproj_kernel(of_ref, os_ref, x_ref, wo1_ref, wo2_ref, ga_ref, lg_ref, lb_ref, sc_ref, sh_ref,
                    wr_ref, br_ref, x1_ref, u_ref, re_ref, rw_ref, *, alpha):
    y = _dot(of_ref[0], wo1_ref[...]) + _dot(os_ref[0], wo2_ref[...])
    x1 = _layer_norm(alpha * x_ref[0] + (1.0 + ga_ref[0]) * y, lg_ref[...], lb_ref[...])
    x1_ref[0] = x1
    u = x1 * (1.0 + sc_ref[0]) + sh_ref[0]
    for c in range(ROW_TILE):
        u_ref[:, c, :] = u[:, c * LANES:(c + 1) * LANES]
    u_hi, u_lo = _split2(u)
    wr = wr_ref[...]
    a = _dot_nt(wr, u_hi)
    b = _dot_nt(wr[0:N_EXPERTS], u_lo)
    logits = a[0:N_EXPERTS] + (a[N_EXPERTS:2 * N_EXPERTS] + b)
    e, w = _route(logits, br_ref[...])
    re_ref[...] = e
    rw_ref[...] = w


def _output_projection(o_fox, o_sb, x, wo1, wo2, g_a, ln_g, ln_b, sc_f, sh_f, wr, br, alpha, ts):
    bsz, s, d = x.shape
    t = bsz * s
    nt = s // ts
    full = lambda shape: pl.BlockSpec(shape, lambda b, i: (0,) * len(shape))
    vec = pl.BlockSpec((1, 1, d), lambda b, i: (b, 0, 0))
    return pl.pallas_call(
        functools.partial(_outproj_kernel, alpha=alpha),
        out_shape=(jax.ShapeDtypeStruct((bsz, s, d), F32),
                   jax.ShapeDtypeStruct((t, ROW_TILE, LANES), F32),
                   jax.ShapeDtypeStruct((2, t), jnp.int32),
                   jax.ShapeDtypeStruct((2, t), F32)),
        grid=(bsz, nt),
        in_specs=[pl.BlockSpec((1, ts, o_fox.shape[2]), lambda b, i: (b, i, 0)),
                  pl.BlockSpec((1, ts, o_sb.shape[2]), lambda b, i: (b, i, 0)),
                  pl.BlockSpec((1, ts, d), lambda b, i: (b, i, 0)),
                  full(wo1.shape), full(wo2.shape), vec, full(ln_g.shape), full(ln_b.shape), vec, vec,
                  full(wr.shape), full(br.shape)],
        out_specs=(pl.BlockSpec((1, ts, d), lambda b, i: (b, i, 0)),
                   pl.BlockSpec((ts, ROW_TILE, LANES), lambda b, i: (b * nt + i, 0, 0)),
                   pl.BlockSpec((2, ts), lambda b, i: (0, b * nt + i)),
                   pl.BlockSpec((2, ts), lambda b, i: (0, b * nt + i))),
        compiler_params=_params(("parallel", "parallel")),
        name="output_projection",
    )(o_fox, o_sb, x, wo1, wo2, g_a, ln_g, ln_b, sc_f, sh_f, wr, br)


def _rank_kernel(e_ref, rank_ref, cnt_ref, carry_ref):
    @pl.when(pl.program_id(0) == 0)
    def _():
        carry_ref[...] = jnp.zeros_like(carry_ref)

    e = e_ref[...]
    ts = e.shape[1]
    eid = lax.broadcasted_iota(jnp.int32, (N_EXPERTS, ts), 0)
    hit0 = eid == e[0:1]
    hit1 = eid == e[1:2]
    member = jnp.where(jnp.logical_or(hit0, hit1), 1.0, 0.0)
    r = lax.broadcasted_iota(jnp.int32, (ts, ts), 0)
    c = lax.broadcasted_iota(jnp.int32, (ts, ts), 1)
    before = jnp.where(r < c, 1.0, 0.0).astype(BF16)
    seen = _dot(member.astype(BF16), before) + carry_ref[:, 0:1]
    r0 = jnp.sum(jnp.where(hit0, seen, 0.0), axis=0, keepdims=True)
    r1 = jnp.sum(jnp.where(hit1, seen, 0.0), axis=0, keepdims=True)
    rank_ref[...] = jnp.concatenate([r0, r1], axis=0).astype(jnp.int32)
    carry_ref[...] = carry_ref[...] + jnp.sum(member, axis=1, keepdims=True)
    cnt_ref[...] = carry_ref[...]


def _expert_ranks(route_e, ts):
    t = route_e.shape[1]
    return pl.pallas_call(
        _rank_kernel,
        out_shape=(jax.ShapeDtypeStruct((2, t), jnp.int32),
                   jax.ShapeDtypeStruct((N_EXPERTS, LANES), F32)),
        grid=(t // ts,),
        in_specs=[pl.BlockSpec((2, ts), lambda i: (0, i))],
        out_specs=(pl.BlockSpec((2, ts), lambda i: (0, i)),
                   pl.BlockSpec((N_EXPERTS, LANES), lambda i: (0, 0))),
        scratch_shapes=[pltpu.VMEM((N_EXPERTS, LANES), F32)],
        compiler_params=_params(("arbitrary",)),
        name="expert_ranks",
    )(route_e)


def _row_copy(src, dst, sem):
    return pltpu.make_async_copy(src, dst, sem)


def _dispatch_kernel(dest_ref, u_ref, xs_in_ref, xs_ref, sem, *, td, t):
    del xs_in_ref
    base = pl.program_id(0) * td

    def start(r, c):
        _row_copy(u_ref.at[r], xs_ref.at[dest_ref[base + r]], sem).start()
        _row_copy(u_ref.at[r], xs_ref.at[dest_ref[t + base + r]], sem).start()
        return c

    lax.fori_loop(0, td, start, 0, unroll=8)

    def wait(r, c):
        _row_copy(u_ref.at[0], xs_ref.at[0], sem).wait()
        _row_copy(u_ref.at[0], xs_ref.at[0], sem).wait()
        return c

    lax.fori_loop(0, td, wait, 0, unroll=8)


def _dispatch(dest, u_rows, p_rows, td):
    t = u_rows.shape[0]
    zeros = jnp.zeros((p_rows, ROW_TILE, LANES), F32)
    return pl.pallas_call(
        functools.partial(_dispatch_kernel, td=td, t=t),
        out_shape=jax.ShapeDtypeStruct((p_rows, ROW_TILE, LANES), F32),
        grid_spec=pltpu.PrefetchScalarGridSpec(
            num_scalar_prefetch=1,
            grid=(t // td,),
            in_specs=[pl.BlockSpec((td, ROW_TILE, LANES), lambda i, d: (i, 0, 0)),
                      pl.BlockSpec(memory_space=pl.ANY)],
            out_specs=pl.BlockSpec(memory_space=pl.ANY),
            scratch_shapes=[pltpu.SemaphoreType.DMA(())]),
        input_output_aliases={2: 0},
        compiler_params=_params(("arbitrary",)),
        name="moe_dispatch",
    )(dest.reshape(-1), u_rows, zeros)


def _expert_kernel(be_ref, nu_ref, x_ref, wg_ref, wu_ref, wd_ref, y_ref, wgb, wub, wdb):
    i = pl.program_id(0)
    prev = be_ref[jnp.maximum(i - 1, 0)]
    fresh = jnp.logical_or(i == 0, be_ref[i] != prev)

    @pl.when(jnp.logical_and(fresh, i < nu_ref[0]))
    def _():
        wgb[...] = wg_ref[0].astype(BF16)
        wub[...] = wu_ref[0].astype(BF16)
        wdb[...] = wd_ref[0].astype(BF16)

    @pl.when(i < nu_ref[0])
    def _():
        x = jnp.concatenate([x_ref[:, c, :] for c in range(ROW_TILE)], axis=1).astype(BF16)
        g = _dot(x, wgb[...])
        u = _dot(x, wub[...])
        h = (g * jax.nn.sigmoid(g) * u).astype(BF16)
        y = _dot(h, wdb[...])
        for c in range(ROW_TILE):
            y_ref[:, c, :] = y[:, c * LANES:(c + 1) * LANES]

    @pl.when(i >= nu_ref[0])
    def _():
        y_ref[...] = jnp.zeros_like(y_ref)


def _experts(block_e, n_used, xs, w_gate, w_up, w_down, layer, bm):
    p_rows = xs.shape[0]
    d, de = w_gate.shape[2], w_gate.shape[3]
    n_blocks = p_rows // bm
    row_map = lambda i, be, nu: (jnp.minimum(i, nu[0] - 1), 0, 0)
    w_map = lambda i, be, nu: (layer, be[i], 0, 0)
    return pl.pallas_call(
        _expert_kernel,
        out_shape=jax.ShapeDtypeStruct((p_rows, ROW_TILE, LANES), F32),
        grid_spec=pltpu.PrefetchScalarGridSpec(
            num_scalar_prefetch=2,
            grid=(n_blocks,),
            in_specs=[pl.BlockSpec((bm, ROW_TILE, LANES), row_map),
                      pl.BlockSpec((None, 1, d, de), w_map),
                      pl.BlockSpec((None, 1, d, de), w_map),
                      pl.BlockSpec((None, 1, de, d), w_map)],
            out_specs=pl.BlockSpec((bm, ROW_TILE, LANES), lambda i, be, nu: (i, 0, 0)),
            scratch_shapes=[pltpu.VMEM((d, de), BF16), pltpu.VMEM((d, de), BF16), pltpu.VMEM((de, d), BF16)]),
        compiler_params=_params(("arbitrary",)),
        name="moe_experts",
    )(block_e, n_used, xs, w_gate, w_up, w_down)


def _combine_kernel(dest_ref, yb_ref, x_ref, w_ref, gf_ref, lg_ref, lb_ref, o_ref, ybuf, sem, *, tc, t, nt, alpha):
    base = (pl.program_id(0) * nt + pl.program_id(1)) * tc

    def start(r, c):
        _row_copy(yb_ref.at[dest_ref[base + r]], ybuf.at[0, r], sem).start()
        _row_copy(yb_ref.at[dest_ref[t + base + r]], ybuf.at[1, r], sem).start()
        return c

    lax.fori_loop(0, tc, start, 0, unroll=8)

    def wait(r, c):
        _row_copy(yb_ref.at[0], ybuf.at[0, 0], sem).wait()
        _row_copy(yb_ref.at[0], ybuf.at[1, 0], sem).wait()
        return c

    lax.fori_loop(0, tc, wait, 0, unroll=8)

    w = w_ref[...]
    y0 = jnp.concatenate([ybuf[0, :, c, :] for c in range(ROW_TILE)], axis=1)
    y1 = jnp.concatenate([ybuf[1, :, c, :] for c in range(ROW_TILE)], axis=1)
    y = y0 * w[:, 0:1] + y1 * w[:, 1:2]
    o_ref[0] = _layer_norm(alpha * x_ref[0] + (1.0 + gf_ref[0]) * y, lg_ref[...], lb_ref[...])


def _combine(dest, yb, x1, w_cols, g_f, ln_g, ln_b, alpha, tc):
    bsz, s, d = x1.shape
    t = bsz * s
    nt = s // tc
    return pl.pallas_call(
        functools.partial(_combine_kernel, tc=tc, t=t, nt=nt, alpha=alpha),
        out_shape=jax.ShapeDtypeStruct((bsz, s, d), F32),
        grid_spec=pltpu.PrefetchScalarGridSpec(
            num_scalar_prefetch=1,
            grid=(bsz, nt),
            in_specs=[pl.BlockSpec(memory_space=pl.ANY),
                      pl.BlockSpec((1, tc, d), lambda b, i, dd: (b, i, 0)),
                      pl.BlockSpec((tc, 2), lambda b, i, dd: (b * nt + i, 0)),
                      pl.BlockSpec((1, 1, d), lambda b, i, dd: (b, 0, 0)),
                      pl.BlockSpec((1, d), lambda b, i, dd: (0, 0)),
                      pl.BlockSpec((1, d), lambda b, i, dd: (0, 0))],
            out_specs=pl.BlockSpec((1, tc, d), lambda b, i, dd: (b, i, 0)),
            scratch_shapes=[pltpu.VMEM((2, tc, ROW_TILE, LANES), F32), pltpu.SemaphoreType.DMA(())]),
        compiler_params=_params(("arbitrary", "arbitrary")),
        name="moe_combine",
    )(dest.reshape(-1), yb, x1, w_cols, g_f, ln_g, ln_b)


def _tiles(s, t):
    pick = lambda n, want: want if n % want == 0 else n
    return dict(proj=pick(s, 512), attn=pick(s, 256), rank=pick(t, 1024), disp=pick(t, 512),
                expert=512, comb=pick(s, 256))


def _forward(x, c, w_mod, b_mod, w_in, b_forget, out_norm_g, w_out, ln1_g, ln1_b,
             w_router, b_router, w_gate, w_up, w_down, ln2_g, ln2_b):
    bsz, s, d = x.shape
    depth = w_mod.shape[0]
    t = bsz * s
    nf, ns = N_FOX_HEADS, N_SB_HEADS
    d_fox, d_sb = nf * HEAD_DIM, ns * HEAD_DIM
    alpha = (2 * depth) ** 0.25
    tl = _tiles(s, t)
    bm = tl["expert"]
    n_blocks = (t * 2) // bm + N_EXPERTS
    p_rows = n_blocks * bm

    mod = _modulation(c, w_mod, b_mod)

    o_fk, o_fv, o_fg, o_ff = d_fox, 2 * d_fox, 3 * d_fox, 4 * d_fox
    o_sq = o_ff + nf
    o_sk, o_sv = o_sq + d_sb, o_sq + 2 * d_sb
    scale = HEAD_DIM ** -0.5
    wq = (jnp.concatenate([w_in[:, :, :o_fk], w_in[:, :, o_sq:o_sk]], axis=2) * scale).astype(BF16)
    wkt = jnp.swapaxes(jnp.concatenate([w_in[:, :, o_fk:o_fv], w_in[:, :, o_sk:o_sv]], axis=2), 1, 2).astype(BF16)
    wv = jnp.concatenate([w_in[:, :, o_fv:o_fg], w_in[:, :, o_sv:]], axis=2).astype(BF16)
    wg = w_in[:, :, o_fg:o_ff].astype(BF16)
    wf_t = jnp.swapaxes(w_in[:, :, o_ff:o_sq], 1, 2)
    wf_hi = wf_t.astype(BF16)
    wf_lo = (wf_t - wf_hi.astype(F32)).astype(BF16)
    pad = jnp.zeros((depth, 16 - nf, d), BF16)
    wf = jnp.concatenate([wf_hi, pad, wf_lo, pad], axis=1)
    wo = w_out.astype(BF16)
    wr_t = w_router.T
    wr_hi = wr_t.astype(BF16)
    wr = jnp.concatenate([wr_hi, (wr_t - wr_hi.astype(F32)).astype(BF16)], axis=0)
    br = b_router.reshape(N_EXPERTS, 1)
    gain = out_norm_g.reshape(depth, 1, d_fox + d_sb)

    for l in range(depth):
        m = mod[l].reshape(bsz, N_MOD, 1, d)
        sh_a, sc_a, g_a, sh_f, sc_f, g_f = (m[:, k] for k in range(N_MOD))

        q, kt, v, gate, log_f = _input_projection(
            x, sc_a, sh_a, wq[l], wkt[l], wv[l], wg[l], wf[l], b_forget[l].reshape(nf, 1), tl["proj"])
        f_cum = _forget_cumsum(log_f)
        f_s = f_cum.reshape(bsz, nf // 2, 2, s)
        f_t = jnp.swapaxes(f_s, 2, 3)
        o_fox = _fox_attention(q, kt, v, gate, f_s, f_t, gain[l], tl["attn"])
        o_sb = _sb_attention(q, kt, v, gain[l], tl["attn"])

        x1, u_rows, route_e, route_w = _output_projection(
            o_fox, o_sb, x, wo[l, :d_fox], wo[l, d_fox:], g_a, ln1_g[l].reshape(1, d), ln1_b[l].reshape(1, d),
            sc_f, sh_f, wr, br, alpha, tl["proj"])

        rank, cnt = _expert_ranks(route_e, tl["rank"])
        counts = cnt[:, 0].astype(jnp.int32)
        padded = (counts + bm - 1) // bm * bm
        pad_ends = jnp.cumsum(padded)
        dest = (pad_ends - padded)[route_e] + rank
        block_e = jnp.minimum(jnp.searchsorted(pad_ends, jnp.arange(n_blocks) * bm, side="right"),
                              N_EXPERTS - 1).astype(jnp.int32)
        n_used = (pad_ends[-1:] // bm).astype(jnp.int32)

        xs = _dispatch(dest, u_rows, p_rows, tl["disp"])
        yb = _experts(block_e, n_used, xs, w_gate, w_up, w_down, l, bm)
        x = _combine(dest, yb, x1, route_w.T, g_f, ln2_g[l].reshape(1, d), ln2_b[l].reshape(1, d), alpha, tl["comb"])
    return x


def kernel(x, c, w_mod, b_mod, w_in, b_forget, out_norm_g, w_out, ln1_g, ln1_b,
           w_router, b_router, w_gate, w_up, w_down, ln2_g, ln2_b):
    return _forward(x, c, w_mod, b_mod, w_in, b_forget, out_norm_g, w_out, ln1_g, ln1_b,
                    w_router, b_router, w_gate, w_up, w_down, ln2_g, ln2_b)
```
